```python
import jax
import jax.numpy as jnp
from jax import lax
import numpy as np

D_MODEL = 1024
BATCH = 4
SEQ = 8192
DEPTH = 1

CTX_LEN = 256
GRID_W = 64
CHUNK = 64
EPS = 1e-6
HG_DK = 128
HG_HEADS = D_MODEL // HG_DK
HG_DV = D_MODEL // HG_HEADS
HG_FDIM = HG_HEADS * HG_DK
HG_VDIM = HG_HEADS * HG_DV
ML_HEADS = 4
ML_DV = D_MODEL // ML_HEADS
ML_DK = ML_DV // 2
ML_QKDIM = ML_HEADS * ML_DK
ML_VDIM = ML_HEADS * ML_DV
CONV_W = 5
N_GROUPS = 4
EXPERTS_PER_GROUP = 8
N_EXPERTS = N_GROUPS * EXPERTS_PER_GROUP
TOP_K = 2
D_EXPERT = 512
SEG_SIZES = (HG_FDIM, HG_FDIM, HG_FDIM, HG_VDIM, HG_VDIM,
             ML_QKDIM, ML_QKDIM, ML_VDIM, 2 * ML_HEADS, 2 * ML_HEADS, ML_VDIM,
             D_MODEL, D_MODEL)
SEG_OFFSETS = tuple(int(o) for o in np.cumsum(SEG_SIZES))
D_IN = SEG_OFFSETS[-1]
ML_F_OFFSET = SEG_OFFSETS[8]

kernel_name = 'hybrid_hgrn2_mlstm_hier_moe'


def rms_norm(x, g):
    xf = x.astype(jnp.float32)
    y = xf * lax.rsqrt(jnp.mean(xf * xf, axis=-1, keepdims=True) + EPS)
    return (y * g.astype(jnp.float32)).astype(x.dtype)


def head_rms_norm(y, g, n_heads):
    b, t, ch = y.shape
    yf = y.astype(jnp.float32).reshape(b, t, n_heads, ch // n_heads)
    yf = yf * lax.rsqrt(jnp.mean(yf * yf, axis=-1, keepdims=True) + EPS)
    return (yf.reshape(b, t, ch) * g.astype(jnp.float32)).astype(y.dtype)


def modulation(cvec, w_ada, b_ada):
    m = jax.nn.silu(cvec) @ w_ada + b_ada
    return jnp.split(m, 6, axis=-1)


def to_heads(a, n_heads):
    b, t, ch = a.shape
    return a.reshape(b, t, n_heads, ch // n_heads).transpose(0, 2, 1, 3)


def from_heads(a):
    b, h, t, d = a.shape
    return a.transpose(0, 2, 1, 3).reshape(b, t, h * d)


def to_colmajor(a):
    b, t, ch = a.shape
    rows = t // GRID_W
    return a.reshape(b, rows, GRID_W, ch).transpose(0, 2, 1, 3).reshape(b, t, ch)


def from_colmajor(a):
    b, t, ch = a.shape
    rows = t // GRID_W
    return a.reshape(b, GRID_W, rows, ch).transpose(0, 2, 1, 3).reshape(b, t, ch)


def flip_t(a):
    return jnp.flip(a, axis=2)


def chunked(a):
    nc = a.shape[2] // CHUNK
    a = a.reshape(a.shape[:2] + (nc, CHUNK) + a.shape[3:])
    return jnp.moveaxis(a, 2, 0)


def unchunk(a):
    a = jnp.moveaxis(a, 0, 2)
    return a.reshape(a.shape[:2] + (a.shape[2] * a.shape[3],) + a.shape[4:])


def short_conv(a, w, b):
    ch = a.shape[-1]
    pad = CONV_W // 2
    y = lax.conv_general_dilated(a, w[:, None, :].astype(a.dtype), window_strides=(1,),
                                 padding=[(pad, pad)], dimension_numbers=('NWC', 'WIO', 'NWC'),
                                 feature_group_count=ch)
    return y + b


def gla_scan(q, k, v, logf, s0):
    mask = jnp.tril(jnp.ones((CHUNK, CHUNK), dtype=bool))

    def step(s, inp):
        qc, kc, vc, lc = inp
        b = jnp.cumsum(lc, axis=2)
        diff = b[:, :, :, None, :] - b[:, :, None, :, :]
        w = jnp.exp(jnp.where(mask[:, :, None], diff, -jnp.inf))
        a = jnp.einsum('bhtk,bhtsk,bhsk->bhts', qc, w, kc)
        o = a @ vc + jnp.einsum('bhtk,bhkv->bhtv', qc * jnp.exp(b), s)
        bl = b[:, :, -1]
        s = jnp.exp(bl)[..., None] * s + jnp.einsum('bhsk,bhsv->bhkv', kc * jnp.exp(bl[:, :, None] - b), vc)
        return s, o

    s, o = lax.scan(step, s0, (chunked(q), chunked(k), chunked(v), chunked(logf)))
    return unchunk(o), s


def mlstm_scan(q, k, v, li, lf, state):
    mask = jnp.tril(jnp.ones((CHUNK, CHUNK), dtype=bool))

    def step(carry, inp):
        c_st, n_st, m_st = carry
        qc, kc, vc, ic, fc = inp
        b = jnp.cumsum(fc, axis=-1)
        d = jnp.where(mask, b[..., :, None] - b[..., None, :] + ic[..., None, :], -jnp.inf)
        g = b + m_st[..., None]
        m_t = jnp.maximum(jnp.max(d, axis=-1), g)
        p = jnp.exp(d - m_t[..., None]) * jnp.einsum('bhtk,bhsk->bhts', qc, kc)
        inter = jnp.exp(g - m_t)
        num = p @ vc + inter[..., None] * jnp.einsum('bhtk,bhkv->bhtv', qc, c_st)
        den = jnp.sum(p, axis=-1) + inter * jnp.einsum('bhtk,bhk->bht', qc, n_st)
        h = num / jnp.maximum(jnp.abs(den), jnp.exp(-m_t))[..., None]
        bl = b[..., -1]
        lw = bl[..., None] - b + ic
        m_new = jnp.maximum(bl + m_st, jnp.max(lw, axis=-1))
        decay = jnp.exp(bl + m_st - m_new)
        wk = jnp.exp(lw - m_new[..., None])[..., None] * kc
        c_st = decay[..., None, None] * c_st + jnp.einsum('bhsk,bhsv->bhkv', wk, vc)
        n_st = decay[..., None] * n_st + jnp.sum(wk, axis=2)
        return (c_st, n_st, m_new), h

    state, h = lax.scan(step, state, (chunked(q), chunked(k), chunked(v), chunked(li), chunked(lf)))
    return unchunk(h), state


def hgrn_core(hq, hff, hfb, hi, lb, s_f, s_b):
    f32 = jnp.float32
    q = to_heads(jax.nn.silu(hq.astype(f32)), HG_HEADS)
    v = to_heads(hi.astype(f32), HG_HEADS)

    def gates(hf, lbd):
        z = hf.astype(f32)
        f = lbd + (1.0 - lbd) * jax.nn.sigmoid(z)
        k = (1.0 - lbd) * jax.nn.sigmoid(-z)
        return to_heads(k, HG_HEADS), to_heads(jnp.log(f), HG_HEADS)

    kf, lff = gates(hff, lb[0])
    kb, lfb = gates(hfb, lb[1])
    o_f, s_f = gla_scan(q, kf, v, lff, s_f)
    o_b, s_b = gla_scan(flip_t(q), flip_t(kb), flip_t(v), flip_t(lfb), s_b)
    o = from_heads(o_f + flip_t(o_b))
    return o.astype(hq.dtype), s_f, s_b


def mlstm_core(hq, hk, hv, hi, hf, conv_w, conv_b, st_f, st_b, grid):
    f32 = jnp.float32
    if grid:
        hq, hk, hv, hi, hf = [to_colmajor(a) for a in (hq, hk, hv, hi, hf)]
    qk = jax.nn.silu(short_conv(jnp.concatenate([hq, hk], axis=-1), conv_w, conv_b))
    q = to_heads(qk[..., :ML_QKDIM].astype(f32), ML_HEADS) * (ML_DK ** -0.5)
    k = to_heads(qk[..., ML_QKDIM:].astype(f32), ML_HEADS)
    v = to_heads(hv.astype(f32), ML_HEADS)
    li = hi.astype(f32).transpose(0, 2, 1)
    lf = jax.nn.log_sigmoid(hf.astype(f32)).transpose(0, 2, 1)
    h_f, st_f = mlstm_scan(q, k, v, li[:, :ML_HEADS], lf[:, :ML_HEADS], st_f)
    h_b, st_b = mlstm_scan(flip_t(q), flip_t(k), flip_t(v), flip_t(li[:, ML_HEADS:]),
                           flip_t(lf[:, ML_HEADS:]), st_b)
    h = from_heads(h_f + flip_t(h_b))
    if grid:
        h = from_colmajor(h)
    return h.astype(hq.dtype), st_f, st_b


def token_mixer(h_lat, h_ctx, w_in, b_in, lb, conv_w, conv_b, hg_norm_g, ml_norm_g,
                w_pa, w_pb, w_out, need_ctx):
    f32 = jnp.float32
    p_lat = jnp.split(h_lat @ w_in + b_in, SEG_OFFSETS[:-1], axis=-1)
    p_ctx = jnp.split(h_ctx @ w_in + b_in, SEG_OFFSETS[:-1], axis=-1)
    bsz = h_lat.shape[0]
    s0 = jnp.zeros((bsz, HG_HEADS, HG_DK, HG_DV), f32)
    m0 = (jnp.zeros((bsz, ML_HEADS, ML_DK, ML_DV), f32),
          jnp.zeros((bsz, ML_HEADS, ML_DK), f32),
          jnp.zeros((bsz, ML_HEADS), f32))
    o_ctx, hs_f, hs_b = hgrn_core(p_ctx[0], p_ctx[1], p_ctx[2], p_ctx[3], lb, s0, s0)
    o_lat, _, _ = hgrn_core(p_lat[0], p_lat[1], p_lat[2], p_lat[3], lb, hs_f, hs_b)
    r_ctx, ms_f, ms_b = mlstm_core(p_ctx[5], p_ctx[6], p_ctx[7], p_ctx[8], p_ctx[9],
                                   conv_w, conv_b, m0, m0, False)
    r_lat, _, _ = mlstm_core(p_lat[5], p_lat[6], p_lat[7], p_lat[8], p_lat[9],
                             conv_w, conv_b, ms_f, ms_b, True)

    def merge(parts, o, r):
        y_a = head_rms_norm(o, hg_norm_g, HG_HEADS) * jax.nn.silu(parts[4])
        y_b = head_rms_norm(r, ml_norm_g, ML_HEADS) * jax.nn.sigmoid(parts[10])
        y = jax.nn.sigmoid(parts[11]) * (y_a @ w_pa) + jax.nn.sigmoid(parts[12]) * (y_b @ w_pb)
        return y @ w_out

    out_lat = merge(p_lat, o_lat, r_lat)
    out_ctx = merge(p_ctx, o_ctx, r_ctx) if need_ctx else None
    return out_lat, out_ctx


def hier_moe(h, w_group, b_group, w_router, b_router, w_gate, w_up, w_down):
    bsz, t, d = h.shape
    tok = h.reshape(bsz * t, d)
    n_tok = tok.shape[0]
    g_logits = (tok @ w_group + b_group).astype(jnp.float32)
    g_idx = jnp.argmax(g_logits, axis=-1)
    g_w = jnp.max(jax.nn.softmax(g_logits, axis=-1), axis=-1, keepdims=True)
    e_logits = (tok @ w_router + b_router).astype(jnp.float32).reshape(n_tok, N_GROUPS, EXPERTS_PER_GROUP)
    e_logits = e_logits[jnp.arange(n_tok), g_idx]
    top_v, top_i = lax.top_k(e_logits, TOP_K)
    top_w = jax.nn.softmax(top_v, axis=-1) * g_w
    expert_id = g_idx[:, None] * EXPERTS_PER_GROUP + top_i
    combine = jnp.sum(jax.nn.one_hot(expert_id, N_EXPERTS, dtype=jnp.float32) * top_w[..., None],
                      axis=1).astype(h.dtype)
    out = jnp.zeros_like(tok)
    for e in range(N_EXPERTS):
        a = jax.nn.silu(tok @ w_gate[e]) * (tok @ w_up[e])
        out = out + combine[:, e:e + 1] * (a @ w_down[e])
    return out.reshape(bsz, t, d)


def setup_inputs(seed: int = 0) -> dict:
    key = jax.random.key(seed)
    ks = jax.random.split(key, 32)
    nrm = jax.random.normal
    f32 = jnp.float32
    b_in = 0.02 * nrm(ks[9], (DEPTH, D_IN), f32)
    forget_bias = jnp.tile(jnp.linspace(3.0, 6.0, ML_HEADS), 2)
    b_in = b_in.at[:, ML_F_OFFSET:ML_F_OFFSET + 2 * ML_HEADS].add(forget_bias)
    return {
        'x': nrm(ks[0], (BATCH, SEQ, D_MODEL), f32),
        'c': nrm(ks[1], (BATCH, D_MODEL), f32),
        'ctx': nrm(ks[2], (BATCH, CTX_LEN, D_MODEL), f32),
        'c_ctx': nrm(ks[3], (D_MODEL,), f32),
        'w_ada': 0.5 * D_MODEL ** -0.5 * nrm(ks[4], (DEPTH, D_MODEL, 6 * D_MODEL), f32),
        'b_ada': 0.02 * nrm(ks[5], (DEPTH, 6 * D_MODEL), f32),
        'norm1_g': 1.0 + 0.02 * nrm(ks[6], (DEPTH, D_MODEL), f32),
        'norm2_g': 1.0 + 0.02 * nrm(ks[7], (DEPTH, D_MODEL), f32),
        'w_in': D_MODEL ** -0.5 * nrm(ks[8], (DEPTH, D_MODEL, D_IN), f32),
        'b_in': b_in,
        'hg_lb_logits': 0.5 * nrm(ks[10], (2, DEPTH + 1, HG_FDIM), f32),
        'hg_norm_g': 1.0 + 0.02 * nrm(ks[11], (DEPTH, HG_VDIM), f32),
        'ml_conv_w': CONV_W ** -0.5 * nrm(ks[12], (DEPTH, CONV_W, 2 * ML_QKDIM), f32),
        'ml_conv_b': 0.02 * nrm(ks[13], (DEPTH, 2 * ML_QKDIM), f32),
        'ml_norm_g': 1.0 + 0.02 * nrm(ks[14], (DEPTH, ML_VDIM), f32),
        'w_branch_a': HG_VDIM ** -0.5 * nrm(ks[15], (DEPTH, HG_VDIM, D_MODEL), f32),
        'w_branch_b': ML_VDIM ** -0.5 * nrm(ks[16], (DEPTH, ML_VDIM, D_MODEL), f32),
        'w_out': D_MODEL ** -0.5 * nrm(ks[17], (DEPTH, D_MODEL, D_MODEL), f32),
        'w_group': D_MODEL ** -0.5 * nrm(ks[18], (DEPTH, D_MODEL, N_GROUPS), f32),
        'b_group': 0.01 * nrm(ks[19], (DEPTH, N_GROUPS), f32),
        'w_router': D_MODEL ** -0.5 * nrm(ks[20], (DEPTH, D_MODEL, N_EXPERTS), f32),
        'b_router': 0.01 * nrm(ks[21], (DEPTH, N_EXPERTS), f32),
        'w_gate': D_MODEL ** -0.5 * nrm(ks[22], (DEPTH, N_EXPERTS, D_MODEL, D_EXPERT), f32),
        'w_up': D_MODEL ** -0.5 * nrm(ks[23], (DEPTH, N_EXPERTS, D_MODEL, D_EXPERT), f32),
        'w_down': D_EXPERT ** -0.5 * nrm(ks[24], (DEPTH, N_EXPERTS, D_EXPERT, D_MODEL), f32),
        'final_norm_g': 1.0 + 0.02 * nrm(ks[25], (D_MODEL,), f32),
    }


def reference(x, c, ctx, c_ctx, w_ada, b_ada, norm1_g, norm2_g, w_in, b_in, hg_lb_logits,
              hg_norm_g, ml_conv_w, ml_conv_b, ml_norm_g, w_branch_a, w_branch_b, w_out,
              w_group, b_group, w_router, b_router, w_gate, w_up, w_down, final_norm_g):
    lb_all = jnp.cumsum(jax.nn.softmax(hg_lb_logits.astype(jnp.float32), axis=1), axis=1)
    for l in range(DEPTH):
        need_ctx = l < DEPTH - 1
        sh1, sc1, g1, sh2, sc2, g2 = [m[:, None, :] for m in modulation(c, w_ada[l], b_ada[l])]
        csh1, csc1, cg1, csh2, csc2, cg2 = modulation(c_ctx, w_ada[l], b_ada[l])
        h_lat = rms_norm(x, norm1_g[l]) * (1.0 + sc1) + sh1
        h_ctx = rms_norm(ctx, norm1_g[l]) * (1.0 + csc1) + csh1
        mix_lat, mix_ctx = token_mixer(h_lat, h_ctx, w_in[l], b_in[l], lb_all[:, l], ml_conv_w[l],
                                       ml_conv_b[l], hg_norm_g[l], ml_norm_g[l], w_branch_a[l],
                                       w_branch_b[l], w_out[l], need_ctx)
        x = x + g1 * mix_lat
        h2 = rms_norm(x, norm2_g[l]) * (1.0 + sc2) + sh2
        x = x + g2 * hier_moe(h2, w_group[l], b_group[l], w_router[l], b_router[l],
                              w_gate[l], w_up[l], w_down[l])
        if need_ctx:
            ctx = ctx + cg1 * mix_ctx
            hc2 = rms_norm(ctx, norm2_g[l]) * (1.0 + csc2) + csh2
            ctx = ctx + cg2 * hier_moe(hc2, w_group[l], b_group[l], w_router[l], b_router[l],
                                       w_gate[l], w_up[l], w_down[l])
    return rms_norm(x, final_norm_g)
```

```python
import functools

import jax
import jax.numpy as jnp
from jax import lax
from jax.experimental import pallas as pl
from jax.experimental.pallas import tpu as pltpu

F32 = jnp.float32
BF16 = jnp.bfloat16
EPS = 1e-6
LANES = 128
SUBLANES = 8

D_MODEL = 1024
GRID_W = 64
CHUNK = 128
HG_HEADS = 8
HG_DK = 128
ML_HEADS = 4
ML_DK = 128
ML_DV = 256
ML_QKDIM = ML_HEADS * ML_DK
CONV_W = 5
N_GROUPS = 4
EXPERTS_PER_GROUP = 8
N_EXPERTS = N_GROUPS * EXPERTS_PER_GROUP
D_EXPERT = 512
HG_SEGS = 5
VMEM_LIMIT = 48 * 1024 * 1024


def _cparams(n_axes):
    return pltpu.CompilerParams(dimension_semantics=("arbitrary",) * n_axes,
                                vmem_limit_bytes=VMEM_LIMIT)


def _sigmoid(x):
    return 1.0 / (1.0 + jnp.exp(-x))


def _silu(x):
    return x * _sigmoid(x)


def _log_sigmoid(x):
    return jnp.minimum(x, 0.0) - jnp.log(1.0 + jnp.exp(-jnp.abs(x)))


def _dot(a, b):
    return jnp.dot(a, b, preferred_element_type=F32)


def _dot_nt(a, b):
    return lax.dot_general(a, b, (((1,), (1,)), ((), ())), preferred_element_type=F32)


def _dot_hi(a, b):
    return jnp.dot(a, b, preferred_element_type=F32, precision=lax.Precision.HIGHEST)


def _rms(x, g):
    ms = jnp.mean(x * x, axis=-1, keepdims=True)
    return x * lax.rsqrt(ms + EPS) * g


def _mod_kernel(c_ref, w_ref, b_ref, o_ref):
    o_ref[...] = _dot_hi(_silu(c_ref[...]), w_ref[...]) + b_ref[...]


def _modulation(c_all, w_ada, b_ada):
    n = w_ada.shape[1]
    tn = 1024
    return pl.pallas_call(
        _mod_kernel,
        out_shape=jax.ShapeDtypeStruct((c_all.shape[0], n), F32),
        grid=(n // tn,),
        in_specs=[pl.BlockSpec(c_all.shape, lambda j: (0, 0)),
                  pl.BlockSpec((D_MODEL, tn), lambda j: (0, j)),
                  pl.BlockSpec((1, tn), lambda j: (0, j))],
        out_specs=pl.BlockSpec((c_all.shape[0], tn), lambda j: (0, j)),
        compiler_params=_cparams(1),
        name="modulation",
    )(c_all, w_ada, b_ada)


def _inproj_kernel(x_ref, sc_ref, sh_ref, g_ref, w_ref, b_ref, *rest, slab, gates):
    if gates:
        wg_ref, bg_ref, p_ref, gate_ref, h_scr = rest
    else:
        p_ref, h_scr = rest

    @pl.when(pl.program_id(2) == 0)
    def _():
        h = _rms(x_ref[...], g_ref[...]) * (1.0 + sc_ref[...]) + sh_ref[...]
        hb = h.astype(BF16)
        h_scr[...] = hb
        if gates:
            gate_ref[...] = _dot(hb, wg_ref[...]) + bg_ref[...]

    res = _dot(h_scr[...], w_ref[...]) + b_ref[...]
    if slab:
        for s in range(res.shape[1] // LANES):
            p_ref[s] = res[:, s * LANES:(s + 1) * LANES].astype(BF16)
    else:
        p_ref[...] = res.astype(BF16)


def _inproj(x, mod3, mod_row, norm_g, w, b, *, slab, w_gates=None, b_gates=None, tm, name):
    bsz, t, d = x.shape
    n = w.shape[1]
    tn = 1024
    gates = w_gates is not None
    in_specs = [
        pl.BlockSpec((None, tm, d), lambda bi, i, j: (bi, i, 0)),
        pl.BlockSpec((None, 1, d), lambda bi, i, j: (mod_row(bi), 0, 1)),
        pl.BlockSpec((None, 1, d), lambda bi, i, j: (mod_row(bi), 0, 0)),
        pl.BlockSpec((1, d), lambda bi, i, j: (0, 0)),
        pl.BlockSpec((d, tn), lambda bi, i, j: (0, j)),
        pl.BlockSpec((1, tn), lambda bi, i, j: (0, j)),
    ]
    args = [x, mod3, mod3, norm_g, w, b]
    if slab:
        out_shape = [jax.ShapeDtypeStruct((bsz, n // LANES, t, LANES), BF16)]
        out_specs = [pl.BlockSpec((None, tn // LANES, tm, LANES), lambda bi, i, j: (bi, j, i, 0))]
    else:
        out_shape = [jax.ShapeDtypeStruct((bsz, t, n), BF16)]
        out_specs = [pl.BlockSpec((None, tm, tn), lambda bi, i, j: (bi, i, j))]
    if gates:
        in_specs += [pl.BlockSpec((d, LANES), lambda bi, i, j: (0, 0)),
                     pl.BlockSpec((1, LANES), lambda bi, i, j: (0, 0))]
        args += [w_gates, b_gates]
        out_shape.append(jax.ShapeDtypeStruct((bsz, t, LANES), F32))
        out_specs.append(pl.BlockSpec((None, tm, LANES), lambda bi, i, j: (bi, i, 0)))
    outs = pl.pallas_call(
        functools.partial(_inproj_kernel, slab=slab, gates=gates),
        out_shape=out_shape,
        grid=(bsz, t // tm, n // tn),
        in_specs=in_specs,
        out_specs=out_specs,
        scratch_shapes=[pltpu.VMEM((tm, d), BF16)],
        compiler_params=_cparams(3),
        name=name,
    )(*args)
    return outs if gates else outs[0]


def _hgrn_chunk(qr, z, v, lb, st, fwd):
    n = qr.shape[0]
    sig = _sigmoid(z)
    oml = 1.0 - lb
    lf = jnp.log(lb + oml * sig)
    k = oml * (1.0 - sig)
    q = _silu(qr)
    row = lax.broadcasted_iota(jnp.int32, (n, LANES), 0)
    ti = lax.broadcasted_iota(jnp.int32, (n, n), 0)
    si = lax.broadcasted_iota(jnp.int32, (n, n), 1)
    txs = ti ^ si
    a = jnp.where(ti == si, _dot_nt(q.astype(BF16), k.astype(BF16)), 0.0)
    p = lf
    tot = lf
    c = 1
    while c < n:
        upper = (row & c) != 0
        z1 = p if fwd else p - lf
        e = jnp.exp(jnp.where(upper, z1, tot - z1))
        if fwd:
            qc = jnp.where(upper, q * e, 0.0)
            kc = jnp.where(upper, 0.0, k * e)
        else:
            qc = jnp.where(upper, 0.0, q * e)
            kc = jnp.where(upper, k * e, 0.0)
        prod = _dot_nt(qc.astype(BF16), kc.astype(BF16))
        if 2 * c < n:
            prod = jnp.where(txs < 2 * c, prod, 0.0)
        a = a + prod
        tot_dn = pltpu.roll(tot, c, 0)
        tot_up = pltpu.roll(tot, n - c, 0)
        p = p + jnp.where(upper, tot_dn, 0.0)
        tot = tot + jnp.where(upper, tot_dn, tot_up)
        c *= 2
    z1 = p if fwd else p - lf
    z2 = tot - z1
    if fwd:
        qi = q * jnp.exp(z1)
        ki = k * jnp.exp(z2)
    else:
        qi = q * jnp.exp(z2)
        ki = k * jnp.exp(z1)
    dec = jnp.exp(tot[0:1, :])
    o = _dot(a.astype(BF16), v.astype(BF16)) + _dot_nt(qi.astype(BF16), st.astype(BF16))
    st_new = dec * st + _dot(v.T.astype(BF16), ki.astype(BF16))
    return o, st_new


def _hgrn_kernel(qf_ref, zf_ref, vf_ref, qb_ref, zb_ref, vb_ref, lbl_ref, s0_ref,
                 of_ref, ob_ref, sfin_ref, st_scr, *, hb, nc):
    c = pl.program_id(2)

    @pl.when(c == 0)
    def _():
        st_scr[...] = s0_ref[...]

    for h in range(hb):
        for d in range(2):
            l0 = lbl_ref[d, 0, h]
            l1 = lbl_ref[d, 1, h]
            mx = jnp.maximum(l0, l1)
            e0 = jnp.exp(l0 - mx)
            lb = e0 / (e0 + jnp.exp(l1 - mx))
            q_ref, z_ref, v_ref, o_ref = ((qf_ref, zf_ref, vf_ref, of_ref) if d == 0
                                          else (qb_ref, zb_ref, vb_ref, ob_ref))
            o, st_new = _hgrn_chunk(q_ref[h].astype(F32), z_ref[h].astype(F32),
                                    v_ref[h].astype(F32), lb, st_scr[d, h], d == 0)
            st_scr[d, h] = st_new
            o_ref[h] = o.astype(BF16)

    @pl.when(c == nc - 1)
    def _():
        sfin_ref[...] = st_scr[...]


def _hgrn(p_hg, lbl, s0, *, hb, name):
    bsz, _, t, _ = p_hg.shape
    nc = t // CHUNK
    ng = HG_HEADS // hb

    def seg(s, rev):
        if rev:
            return pl.BlockSpec((None, hb, CHUNK, LANES), lambda b, g, c: (b, s * ng + g, nc - 1 - c, 0))
        return pl.BlockSpec((None, hb, CHUNK, LANES), lambda b, g, c: (b, s * ng + g, c, 0))

    st_spec = pl.BlockSpec((None, 2, hb, HG_DK, HG_DK), lambda b, g, c: (b, 0, g, 0, 0))
    return pl.pallas_call(
        functools.partial(_hgrn_kernel, hb=hb, nc=nc),
        out_shape=[jax.ShapeDtypeStruct((bsz, HG_HEADS, t, LANES), BF16),
                   jax.ShapeDtypeStruct((bsz, HG_HEADS, t, LANES), BF16),
                   jax.ShapeDtypeStruct((bsz, 2, HG_HEADS, HG_DK, HG_DK), F32)],
        grid=(bsz, ng, nc),
        in_specs=[seg(0, False), seg(1, False), seg(3, False),
                  seg(0, True), seg(2, True), seg(3, True),
                  pl.BlockSpec((2, 2, hb, 1, LANES), lambda b, g, c: (0, 0, g, 0, 0)),
                  st_spec],
        out_specs=[seg(0, False), seg(0, True), st_spec],
        scratch_shapes=[pltpu.VMEM((2, hb, HG_DK, HG_DK), F32)],
        compiler_params=_cparams(3),
        name=name,
    )(p_hg, p_hg, p_hg, p_hg, p_hg, p_hg, lbl, s0)


def _conv_kernel(main_ref, prev_ref, next_ref, w_ref, b_ref, o_ref, ext_scr, *, nc):
    c = pl.program_id(1)
    n = main_ref.shape[0]
    pad = CONV_W // 2
    ext_scr[0:SUBLANES, :] = jnp.where(c > 0, prev_ref[...].astype(F32), 0.0)
    ext_scr[SUBLANES:SUBLANES + n, :] = main_ref[...].astype(F32)
    ext_scr[SUBLANES + n:2 * SUBLANES + n, :] = jnp.where(c < nc - 1, next_ref[...].astype(F32), 0.0)
    acc = b_ref[...]
    for tap in range(CONV_W):
        acc = acc + w_ref[tap:tap + 1, :] * ext_scr[SUBLANES - pad + tap:SUBLANES - pad + tap + n, :]
    lane = lax.broadcasted_iota(jnp.int32, (1, 2 * ML_QKDIM), 1)
    y = _silu(acc) * jnp.where(lane < ML_QKDIM, ML_DK ** -0.5, 1.0)
    for s in range(2 * ML_HEADS):
        o_ref[s] = y[:, s * LANES:(s + 1) * LANES].astype(BF16)


def _ml_conv(p_ml, conv_w, conv_b, *, colmajor, name):
    bsz, t, width = p_ml.shape
    qk = 2 * ML_QKDIM
    if colmajor:
        rows = t // GRID_W
        nc = GRID_W
        src = p_ml.reshape(bsz, rows, GRID_W * width)
        nb = width // qk
        main = pl.BlockSpec((None, rows, qk), lambda b, c: (b, 0, nb * c))
        prev = pl.BlockSpec((None, SUBLANES, qk),
                            lambda b, c: (b, rows // SUBLANES - 1, nb * jnp.maximum(c - 1, 0)))
        nxt = pl.BlockSpec((None, SUBLANES, qk), lambda b, c: (b, 0, nb * jnp.minimum(c + 1, nc - 1)))
    else:
        rows = CHUNK
        nc = t // rows
        src = p_ml
        rb = rows // SUBLANES
        main = pl.BlockSpec((None, rows, qk), lambda b, c: (b, c, 0))
        prev = pl.BlockSpec((None, SUBLANES, qk), lambda b, c: (b, jnp.maximum(c * rb - 1, 0), 0))
        nxt = pl.BlockSpec((None, SUBLANES, qk),
                           lambda b, c: (b, jnp.minimum((c + 1) * rb, t // SUBLANES - 1), 0))
    return pl.pallas_call(
        functools.partial(_conv_kernel, nc=nc),
        out_shape=jax.ShapeDtypeStruct((bsz, 2 * ML_HEADS, t, LANES), BF16),
        grid=(bsz, nc),
        in_specs=[main, prev, nxt,
                  pl.BlockSpec((CONV_W, qk), lambda b, c: (0, 0)),
                  pl.BlockSpec((1, qk), lambda b, c: (0, 0))],
        out_specs=pl.BlockSpec((None, 2 * ML_HEADS, rows, LANES), lambda b, c: (b, 0, c, 0)),
        scratch_shapes=[pltpu.VMEM((rows + 2 * SUBLANES, qk), F32)],
        compiler_params=_cparams(2),
        name=name,
    )(src, src, src, conv_w, conv_b)


def _mlstm_kernel(qkf_ref, vf_ref, gf_ref, qkb_ref, vb_ref, gb_ref, c0_ref, n0_ref, m0_ref,
                  rf_ref, rb_ref, cfin_ref, nfin_ref, mfin_ref, c_scr, n_scr, m_scr, *, nc):
    step = pl.program_id(1)

    @pl.when(step == 0)
    def _():
        c_scr[...] = c0_ref[...]
        n_scr[...] = n0_ref[...]
        m_scr[...] = m0_ref[...]

    n = gf_ref.shape[0]
    ti = lax.broadcasted_iota(jnp.int32, (n, n), 0)
    si = lax.broadcasted_iota(jnp.int32, (n, n), 1)
    neg_inf = -jnp.inf
    for d in range(2):
        qk_ref, v_ref, g_ref, r_ref = ((qkf_ref, vf_ref, gf_ref, rf_ref) if d == 0
                                       else (qkb_ref, vb_ref, gb_ref, rb_ref))
        causal = (si <= ti) if d == 0 else (si >= ti)
        g = g_ref[...]
        g_t = g.T
        cum = _dot_hi(jnp.where(causal, 1.0, 0.0), _log_sigmoid(g))
        cum_t = _dot_hi(_log_sigmoid(g_t), jnp.where(causal, 0.0, 1.0) + jnp.where(ti == si, 1.0, 0.0))
        for h in range(ML_HEADS):
            gi = d * ML_HEADS + h
            fi = 2 * ML_HEADS + gi
            alpha = cum[:, fi:fi + 1]
            beta_col = g[:, gi:gi + 1] - alpha
            beta_row = g_t[gi:gi + 1, :] - cum_t[fi:fi + 1, :]
            tau = alpha[n - 1:n, :] if d == 0 else alpha[0:1, :]
            m_prev = m_scr[gi][:, 0:1]
            dm = jnp.where(causal, alpha + beta_row, neg_inf)
            gg = alpha + m_prev
            m_t = jnp.maximum(jnp.max(dm, axis=-1, keepdims=True), gg)
            qh = qk_ref[h]
            kh = qk_ref[ML_HEADS + h]
            vh = v_ref[:, h * ML_DV:(h + 1) * ML_DV]
            p = jnp.exp(dm - m_t) * _dot_nt(qh, kh)
            inter = jnp.exp(gg - m_t)
            c_st = c_scr[gi]
            n_st = n_scr[gi]
            num = _dot(p.astype(BF16), vh) + inter * _dot(qh, c_st.astype(BF16))
            den = (jnp.sum(p, axis=-1, keepdims=True)
                   + inter * jnp.sum(qh.astype(F32) * n_st, axis=-1, keepdims=True))
            hout = num / jnp.maximum(jnp.abs(den), jnp.exp(-m_t))
            r_ref[:, h * ML_DV:(h + 1) * ML_DV] = hout.astype(BF16)
            m_new = jnp.maximum(tau + m_prev, jnp.max(tau + beta_row, axis=-1, keepdims=True))
            decay = jnp.exp(tau + m_prev - m_new)
            wk = jnp.exp(tau + beta_col - m_new) * kh.astype(F32)
            c_scr[gi] = decay * c_st + _dot(wk.T.astype(BF16), vh)
            n_scr[gi] = decay * n_st + jnp.sum(wk, axis=0, keepdims=True)
            m_scr[gi] = jnp.broadcast_to(m_new, (1, LANES))

    @pl.when(step == nc - 1)
    def _():
        cfin_ref[...] = c_scr[...]
        nfin_ref[...] = n_scr[...]
        mfin_ref[...] = m_scr[...]


def _mlstm(qk_slab, p_ml, gates, state, *, colmajor, name):
    bsz, t, width = p_ml.shape
    vdim = ML_HEADS * ML_DV
    nh2 = 2 * ML_HEADS
    if colmajor:
        rows = t // GRID_W
        nc = GRID_W
        nb = width // vdim
        v_src = p_ml.reshape(bsz, rows, GRID_W * width)
        g_src = gates.reshape(bsz, rows, GRID_W * LANES)
        out_sds = jax.ShapeDtypeStruct((bsz, rows, GRID_W * vdim), BF16)

        def vspec(rev):
            return pl.BlockSpec((None, rows, vdim),
                                lambda b, c: (b, 0, nb * ((nc - 1 - c) if rev else c) + nb - 1))

        def gspec(rev):
            return pl.BlockSpec((None, rows, LANES), lambda b, c: (b, 0, (nc - 1 - c) if rev else c))

        def ospec(rev):
            return pl.BlockSpec((None, rows, vdim), lambda b, c: (b, 0, (nc - 1 - c) if rev else c))
    else:
        rows = CHUNK
        nc = t // rows
        nb = width // vdim
        v_src = p_ml
        g_src = gates
        out_sds = jax.ShapeDtypeStruct((bsz, t, vdim), BF16)

        def vspec(rev):
            return pl.BlockSpec((None, rows, vdim), lambda b, c: (b, (nc - 1 - c) if rev else c, nb - 1))

        def gspec(rev):
            return pl.BlockSpec((None, rows, LANES), lambda b, c: (b, (nc - 1 - c) if rev else c, 0))

        def ospec(rev):
            return pl.BlockSpec((None, rows, vdim), lambda b, c: (b, (nc - 1 - c) if rev else c, 0))

    def qspec(rev):
        return pl.BlockSpec((None, nh2, rows, LANES), lambda b, c: (b, 0, (nc - 1 - c) if rev else c, 0))

    c_spec = pl.BlockSpec((None, nh2, ML_DK, ML_DV), lambda b, c: (b, 0, 0, 0))
    nm_spec = pl.BlockSpec((None, nh2, 1, LANES), lambda b, c: (b, 0, 0, 0))
    c0, n0, m0 = state
    outs = pl.pallas_call(
        functools.partial(_mlstm_kernel, nc=nc),
        out_shape=[out_sds, out_sds,
                   jax.ShapeDtypeStruct(c0.shape, F32),
                   jax.ShapeDtypeStruct(n0.shape, F32),
                   jax.ShapeDtypeStruct(m0.shape, F32)],
        grid=(bsz, nc),
        in_specs=[qspec(False), vspec(False), gspec(False),
                  qspec(True), vspec(True), gspec(True),
                  c_spec, nm_spec, nm_spec],
        out_specs=[ospec(False), ospec(True), c_spec, nm_spec, nm_spec],
        scratch_shapes=[pltpu.VMEM((nh2, ML_DK, ML_DV), F32),
                        pltpu.VMEM((nh2, 1, LANES), F32),
                        pltpu.VMEM((nh2, 1, LANES), F32)],
        compiler_params=_cparams(2),
        name=name,
    )(qk_slab, v_src, g_src, qk_slab, v_src, g_src, c0, n0, m0)
    r_f, r_b, c_fin, n_fin, m_fin = outs
    return r_f.reshape(bsz, t, vdim), r_b.reshape(bsz, t, vdim), (c_fin, n_fin, m_fin)


def _merge_kernel(of_ref, ob_ref, hgg_ref, rf_ref, rb_ref, mo_ref, ga_ref, gb_ref, x_ref,
                  g1_ref, sc2_ref, sh2_ref, hgn_ref, mln_ref, n2g_ref,
                  wpa_ref, wpb_ref, wout_ref, wr_ref, br_ref,
                  x1_ref, h2_ref, comb_ref):
    ya = []
    for h in range(HG_HEADS):
        sl = slice(h * HG_DK, (h + 1) * HG_DK)
        o = of_ref[h].astype(F32) + ob_ref[h].astype(F32)
        ya.append((_rms(o, hgn_ref[:, sl]) * _silu(hgg_ref[h].astype(F32))).astype(BF16))
    ya = jnp.concatenate(ya, axis=-1)
    yb = []
    for h in range(ML_HEADS):
        sl = slice(h * ML_DV, (h + 1) * ML_DV)
        r = rf_ref[:, sl].astype(F32) + rb_ref[:, sl].astype(F32)
        yb.append((_rms(r, mln_ref[:, sl]) * _sigmoid(mo_ref[:, sl].astype(F32))).astype(BF16))
    yb = jnp.concatenate(yb, axis=-1)
    y = (_sigmoid(ga_ref[...].astype(F32)) * _dot(ya, wpa_ref[...])
         + _sigmoid(gb_ref[...].astype(F32)) * _dot(yb, wpb_ref[...]))
    x1 = x_ref[...] + g1_ref[...] * _dot(y.astype(BF16), wout_ref[...])
    x1_ref[...] = x1
    h2 = _rms(x1, n2g_ref[...]) * (1.0 + sc2_ref[...]) + sh2_ref[...]
    h2_ref[...] = h2.astype(BF16)

    logits = _dot_hi(h2, wr_ref[...]) + br_ref[...]
    lane = lax.broadcasted_iota(jnp.int32, logits.shape, 1)
    lane_f = lane.astype(F32)
    neg_inf = -jnp.inf
    big = 1e9
    gl = jnp.where(lane >= N_EXPERTS, jnp.where(lane < N_EXPERTS + N_GROUPS, logits, neg_inf), neg_inf)
    gmax = jnp.max(gl, axis=-1, keepdims=True)
    gidx = jnp.min(jnp.where(gl == gmax, lane_f, big), axis=-1, keepdims=True) - float(N_EXPERTS)
    g_w = 1.0 / jnp.sum(jnp.exp(gl - gmax), axis=-1, keepdims=True)
    grp_f = (lane // EXPERTS_PER_GROUP).astype(F32)
    el = jnp.where(lane < N_EXPERTS, jnp.where(grp_f == gidx, logits, neg_inf), neg_inf)
    e1 = jnp.max(el, axis=-1, keepdims=True)
    i1 = jnp.min(jnp.where(el == e1, lane_f, big), axis=-1, keepdims=True)
    el2 = jnp.where(lane_f == i1, neg_inf, el)
    e2 = jnp.max(el2, axis=-1, keepdims=True)
    i2 = jnp.min(jnp.where(el2 == e2, lane_f, big), axis=-1, keepdims=True)
    t = jnp.exp(e2 - e1)
    w1 = g_w / (1.0 + t)
    w2 = g_w * t / (1.0 + t)
    comb_ref[...] = jnp.where(lane_f == i1, w1, 0.0) + jnp.where(lane_f == i2, w2, 0.0)


def _merge(o_f, o_b, p_hg, r_f, r_b, p_gt, x, mod3, hg_norm_g, ml_norm_g, norm2_g,
           w_pa, w_pb, w_out, w_route, b_route, *, tm):
    bsz, t, d = x.shape
    ng = HG_HEADS

    def slab(blk):
        return pl.BlockSpec((None, ng, tm, LANES), lambda b, i: (b, blk, i, 0))

    def row(blk):
        return pl.BlockSpec((None, tm, d), lambda b, i: (b, i, blk))

    def modrow(k):
        return pl.BlockSpec((None, 1, d), lambda b, i: (b, 0, k))

    def full(shape):
        return pl.BlockSpec(shape, lambda b, i: (0,) * len(shape))

    return pl.pallas_call(
        _merge_kernel,
        out_shape=[jax.ShapeDtypeStruct((bsz, t, d), F32),
                   jax.ShapeDtypeStruct((bsz, t, d), BF16),
                   jax.ShapeDtypeStruct((bsz, t, LANES), F32)],
        grid=(bsz, t // tm),
        in_specs=[slab(0), slab(0), slab(HG_SEGS - 1), row(0), row(0), row(0), row(1), row(2), row(0),
                  modrow(2), modrow(4), modrow(3),
                  full((1, d)), full((1, d)), full((1, d)),
                  full((d, d)), full((d, d)), full((d, d)), full((d, LANES)), full((1, LANES))],
        out_specs=[row(0), row(0), pl.BlockSpec((None, tm, LANES), lambda b, i: (b, i, 0))],
        compiler_params=_cparams(2),
        name="merge_router",
    )(o_f, o_b, p_hg, r_f, r_b, p_gt, p_gt, p_gt, x, mod3, mod3, mod3,
      hg_norm_g, ml_norm_g, norm2_g, w_pa, w_pb, w_out, w_route, b_route)


def _moe_kernel(h2_ref, comb_ref, x1_ref, g2_ref, wg_ref, wu_ref, wd_ref, fg_ref, o_ref, acc_scr):
    e = pl.program_id(1)

    @pl.when(e == 0)
    def _():
        acc_scr[...] = jnp.zeros_like(acc_scr)

    hb = h2_ref[...]
    a = _silu(_dot(hb, wg_ref[...].astype(BF16))) * _dot(hb, wu_ref[...].astype(BF16))
    y = _dot(a.astype(BF16), wd_ref[...].astype(BF16))
    lane = lax.broadcasted_iota(jnp.int32, comb_ref.shape, 1)
    ce = jnp.sum(jnp.where(lane == e, comb_ref[...], 0.0), axis=-1, keepdims=True)
    acc_scr[...] += ce * y

    @pl.when(e == N_EXPERTS - 1)
    def _():
        x2 = x1_ref[...] + g2_ref[...] * acc_scr[...]
        o_ref[...] = _rms(x2, fg_ref[...])


def _moe_dense(h2, comb, x1, mod3, w_gate, w_up, w_down, final_g, *, tm, tiles_per_batch):
    n, d = h2.shape
    return pl.pallas_call(
        _moe_kernel,
        out_shape=jax.ShapeDtypeStruct((n, d), F32),
        grid=(n // tm, N_EXPERTS),
        in_specs=[pl.BlockSpec((tm, d), lambda i, e: (i, 0)),
                  pl.BlockSpec((tm, LANES), lambda i, e: (i, 0)),
                  pl.BlockSpec((tm, d), lambda i, e: (i, 0)),
                  pl.BlockSpec((None, 1, d), lambda i, e: (i // tiles_per_batch, 0, 5)),
                  pl.BlockSpec((None, d, D_EXPERT), lambda i, e: (e, 0, 0)),
                  pl.BlockSpec((None, d, D_EXPERT), lambda i, e: (e, 0, 0)),
                  pl.BlockSpec((None, D_EXPERT, d), lambda i, e: (e, 0, 0)),
                  pl.BlockSpec((1, d), lambda i, e: (0, 0))],
        out_specs=pl.BlockSpec((tm, d), lambda i, e: (i, 0)),
        scratch_shapes=[pltpu.VMEM((tm, d), F32)],
        compiler_params=_cparams(2),
        name="moe_experts",
    )(h2, comb, x1, mod3, w_gate, w_up, w_down, final_g)


def kernel(x, c, ctx, c_ctx, w_ada, b_ada, norm1_g, norm2_g, w_in, b_in, hg_lb_logits, hg_norm_g,
           ml_conv_w, ml_conv_b, ml_norm_g, w_branch_a, w_branch_b, w_out, w_group, b_group,
           w_router, b_router, w_gate, w_up, w_down, final_norm_g):
    assert w_ada.shape[0] == 1, "single-layer problem"
    bsz, t, d = x.shape
    t_ctx = ctx.shape[1]
    ctx_row = bsz

    c_all = jnp.concatenate([c, c_ctx[None], jnp.zeros((SUBLANES - 1 - bsz, d), F32)], axis=0)
    mod = _modulation(c_all, w_ada[0], b_ada[0][None])
    mod3 = mod.reshape(SUBLANES, 1, 6 * d)

    w = w_in[0]
    b = b_in[0][None]
    o_ml = HG_SEGS * d
    o_gate = o_ml + 2 * ML_QKDIM + ML_HEADS * ML_DV
    o_gt = o_gate + 4 * ML_HEADS
    w_hg, b_hg = w[:, :o_ml].astype(BF16), b[:, :o_ml]
    w_ml, b_ml = w[:, o_ml:o_gate].astype(BF16), b[:, o_ml:o_gate]
    pad = LANES - 4 * ML_HEADS
    w_gates = jnp.pad(w[:, o_gate:o_gt], ((0, 0), (0, pad))).astype(BF16)
    b_gates = jnp.pad(b[:, o_gate:o_gt], ((0, 0), (0, pad)))
    w_gt, b_gt = w[:, o_gt:].astype(BF16), b[:, o_gt:]
    n1g = norm1_g[0][None]

    lat_row = lambda bi: bi
    ctx_rowf = lambda bi: ctx_row
    lbl = hg_lb_logits.reshape(2, 2, HG_HEADS, 1, HG_DK)

    p_hg_ctx = _inproj(ctx, mod3, ctx_rowf, n1g, w_hg, b_hg, slab=True, tm=t_ctx, name="inproj_hg_ctx")
    s_zero = jnp.zeros((bsz, 2, HG_HEADS, HG_DK, HG_DK), F32)
    _, _, s_ctx = _hgrn(p_hg_ctx, lbl, s_zero, hb=2, name="hgrn_ctx")
    p_hg = _inproj(x, mod3, lat_row, n1g, w_hg, b_hg, slab=True, tm=1024, name="inproj_hg")
    o_f, o_b, _ = _hgrn(p_hg, lbl, s_ctx, hb=2, name="hgrn_lat")

    cw = ml_conv_w[0]
    cb = ml_conv_b[0][None]
    p_ml_ctx, g_ctx = _inproj(ctx, mod3, ctx_rowf, n1g, w_ml, b_ml, slab=False, w_gates=w_gates,
                              b_gates=b_gates, tm=t_ctx, name="inproj_ml_ctx")
    qk_ctx = _ml_conv(p_ml_ctx, cw, cb, colmajor=False, name="mlconv_ctx")
    nh2 = 2 * ML_HEADS
    st_zero = (jnp.zeros((bsz, nh2, ML_DK, ML_DV), F32), jnp.zeros((bsz, nh2, 1, LANES), F32),
               jnp.zeros((bsz, nh2, 1, LANES), F32))
    _, _, st_ctx = _mlstm(qk_ctx, p_ml_ctx, g_ctx, st_zero, colmajor=False, name="mlstm_ctx")
    p_ml, g_lat = _inproj(x, mod3, lat_row, n1g, w_ml, b_ml, slab=False, w_gates=w_gates,
                          b_gates=b_gates, tm=1024, name="inproj_ml")
    qk_lat = _ml_conv(p_ml, cw, cb, colmajor=True, name="mlconv_lat")
    r_f, r_b, _ = _mlstm(qk_lat, p_ml, g_lat, st_ctx, colmajor=True, name="mlstm_lat")

    p_gt = _inproj(x, mod3, lat_row, n1g, w_gt, b_gt, slab=False, tm=1024, name="inproj_gt")
    w_route = jnp.pad(jnp.concatenate([w_router[0], w_group[0]], axis=1),
                      ((0, 0), (0, LANES - N_EXPERTS - N_GROUPS)))
    b_route = jnp.pad(jnp.concatenate([b_router[0], b_group[0]], axis=0),
                      (0, LANES - N_EXPERTS - N_GROUPS))[None]
    x1, h2, comb = _merge(o_f, o_b, p_hg, r_f, r_b, p_gt, x, mod3, hg_norm_g[0][None], ml_norm_g[0][None],
                          norm2_g[0][None], w_branch_a[0].astype(BF16), w_branch_b[0].astype(BF16),
                          w_out[0].astype(BF16), w_route, b_route, tm=512)

    tm_moe = 1024
    out = _moe_dense(h2.reshape(bsz * t, d), comb.reshape(bsz * t, LANES), x1.reshape(bsz * t, d), mod3,
                     w_gate[0], w_up[0], w_down[0], final_norm_g[None], tm=tm_moe,
                     tiles_per_batch=t // tm_moe)
    return out.reshape(bsz, t, d)
```
